```python
import math
import jax, jax.numpy as jnp
from jax import lax
import numpy as np

D_MODEL = 1024
BATCH = 2
SEQ = 8192
DEPTH = 4
DEC_BATCH = 32
DEC_SEQ = 1
PAST_LEN = 8192
PAGE_SIZE = 128

N_MIXERS = 3
N_RET = (DEPTH + 2) // 3
N_DIFF = (DEPTH + 1) // 3
N_LRU = DEPTH // 3

RET_HEADS = 4
RET_DK = 256
RET_DV = 512
RET_QK = RET_HEADS * RET_DK
RET_V = RET_HEADS * RET_DV
RET_CHUNK = 128

DIFF_HEADS = 8
DIFF_DK = 64
DIFF_DV = 2 * DIFF_DK
DIFF_QK = DIFF_HEADS * 2 * DIFF_DK
DIFF_V = DIFF_HEADS * DIFF_DV
Q_BLOCK = 128

LRU_WIDTH = 1536
LRU_BLOCKS = 12
LRU_BW = LRU_WIDTH // LRU_BLOCKS
CONV_W = 4
LRU_C = 8.0

NORM_EPS = 1e-6

kernel_name = 'hybrid_ret_diffattn_rglru_step'


def rms_norm(x, gain=None):
    xf = x.astype(jnp.float32)
    y = xf * lax.rsqrt(jnp.mean(xf * xf, axis=-1, keepdims=True) + NORM_EPS)
    if gain is not None:
        y = y * gain.astype(jnp.float32)
    return y.astype(x.dtype)


def ret_log_gamma():
    return jnp.log1p(-jnp.exp2(-5.0 - jnp.arange(RET_HEADS, dtype=jnp.float32)))


def alibi_slopes():
    return jnp.exp2(-8.0 * (jnp.arange(DIFF_HEADS, dtype=jnp.float32) + 1.0) / DIFF_HEADS)


def retention_chunk(S0, qkv, lg):
    q, k, v = qkv
    C = q.shape[1]
    idx = jnp.arange(C, dtype=jnp.float32)
    rel = idx[:, None] - idx[None, :]
    decay = jnp.where(rel >= 0, jnp.exp(jnp.maximum(rel, 0.0)[None] * lg[:, None, None]), 0.0)
    s = jnp.einsum('bihd,bjhd->bhij', q, k) * decay
    o = jnp.einsum('bhij,bjhv->bihv', s, v)
    q_dec = jnp.exp((idx + 1.0)[:, None] * lg[None, :])
    o = o + jnp.einsum('bihd,bhdv->bihv', q, S0) * q_dec[None, :, :, None]
    k_dec = jnp.exp((C - 1.0 - idx)[:, None] * lg[None, :])
    S = jnp.exp(C * lg)[None, :, None, None] * S0 + jnp.einsum('bjhd,bjhv->bhdv', k * k_dec[None, :, :, None], v)
    return S, o


def retention_branch(h, S0, w_in, w_out):
    B, T, _ = h.shape
    q, k, v, g = jnp.split(h @ w_in, [RET_QK, 2 * RET_QK, 2 * RET_QK + RET_V], axis=-1)
    q = q.reshape(B, T, RET_HEADS, RET_DK).astype(jnp.float32)
    k = k.reshape(B, T, RET_HEADS, RET_DK).astype(jnp.float32) * (RET_DK ** -0.5)
    v = v.reshape(B, T, RET_HEADS, RET_DV).astype(jnp.float32)
    C = min(RET_CHUNK, T)
    n = T // C
    to_chunks = lambda a: jnp.moveaxis(a.reshape(B, n, C, *a.shape[2:]), 1, 0)
    lg = ret_log_gamma()
    S, o = lax.scan(lambda S, c: retention_chunk(S, c, lg), S0.astype(jnp.float32),
                    (to_chunks(q), to_chunks(k), to_chunks(v)))
    o = jnp.moveaxis(o, 0, 1).reshape(B, T, RET_HEADS, RET_DV)
    o = rms_norm(o).reshape(B, T, RET_V).astype(h.dtype)
    return (o * jax.nn.silu(g)) @ w_out, S


def diff_project(h, w_in, q_gain, k_gain):
    B, T, _ = h.shape
    q, k, v, g = jnp.split(h @ w_in, [DIFF_QK, 2 * DIFF_QK, 2 * DIFF_QK + DIFF_V], axis=-1)
    q = rms_norm(q.reshape(B, T, DIFF_HEADS, 2, DIFF_DK), q_gain)
    k = rms_norm(k.reshape(B, T, DIFF_HEADS, 2, DIFF_DK), k_gain)
    v = v.reshape(B, T, DIFF_HEADS, DIFF_DV)
    return q, k, v, g


def diff_lambda(lq1, lk1, lq2, lk2, lam_init):
    f = lambda a: a.astype(jnp.float32)
    return jnp.exp(jnp.sum(f(lq1) * f(lk1))) - jnp.exp(jnp.sum(f(lq2) * f(lk2))) + lam_init


def diff_attend_prompt(q, k, v, lam):
    B, T = q.shape[:2]
    nb = T // Q_BLOCK
    scale = DIFF_DK ** -0.5
    slopes = alibi_slopes()
    pos_s = jnp.arange(T, dtype=jnp.float32)
    vf = v.astype(jnp.float32)
    q_blocks = jnp.moveaxis(q.reshape(B, nb, Q_BLOCK, DIFF_HEADS, 2, DIFF_DK), 1, 0)

    def block(args):
        qb, t0 = args
        pos_t = t0 + jnp.arange(Q_BLOCK, dtype=jnp.float32)
        dist = pos_t[:, None] - pos_s[None, :]
        s = jnp.einsum('bthmd,bshmd->bmhts', qb, k).astype(jnp.float32) * scale - slopes[:, None, None] * dist[None]
        s = jnp.where(dist >= 0, s, -jnp.inf)
        p = jax.nn.softmax(s, axis=-1)
        o = jnp.einsum('bmhts,bshv->bmhtv', p, vf)
        return jnp.moveaxis(o[:, 0] - lam * o[:, 1], 1, 2)

    o = lax.map(block, (q_blocks, jnp.arange(nb, dtype=jnp.float32) * Q_BLOCK))
    return jnp.moveaxis(o, 0, 1).reshape(B, T, DIFF_HEADS, DIFF_DV)


def diff_attend_sample(q, k, v, cache_k, cache_v, layer, page_table, lam):
    B, T = q.shape[:2]
    n_pages = page_table.shape[1]
    past = n_pages * PAGE_SIZE
    scale = DIFF_DK ** -0.5
    slopes = alibi_slopes()
    pos_t = past + jnp.arange(T, dtype=jnp.float32)
    dist = (jnp.arange(T)[:, None] - jnp.arange(T)[None, :]).astype(jnp.float32)
    s = jnp.einsum('bthmd,bshmd->bmhts', q, k).astype(jnp.float32) * scale - slopes[:, None, None] * dist[None]
    s = jnp.where(dist >= 0, s, -jnp.inf)
    m = jnp.max(s, axis=-1)
    p = jnp.exp(s - m[..., None])
    l = jnp.sum(p, axis=-1)
    acc = jnp.einsum('bmhts,bshv->bmhtv', p, v.astype(jnp.float32))

    def page_step(carry, xs):
        m, l, acc = carry
        pidx, pt = xs
        kp = cache_k[layer, pt].reshape(B, PAGE_SIZE, DIFF_HEADS, 2, DIFF_DK)
        vp = cache_v[layer, pt].astype(jnp.float32)
        pos_s = pidx.astype(jnp.float32) * PAGE_SIZE + jnp.arange(PAGE_SIZE, dtype=jnp.float32)
        d = pos_t[:, None] - pos_s[None, :]
        s = jnp.einsum('bthmd,bshmd->bmhts', q, kp).astype(jnp.float32) * scale - slopes[:, None, None] * d[None]
        m_new = jnp.maximum(m, jnp.max(s, axis=-1))
        corr = jnp.exp(m - m_new)
        p = jnp.exp(s - m_new[..., None])
        l = l * corr + jnp.sum(p, axis=-1)
        acc = acc * corr[..., None] + jnp.einsum('bmhts,bshv->bmhtv', p, vp)
        return (m_new, l, acc), None

    (m, l, acc), _ = lax.scan(page_step, (m, l, acc), (jnp.arange(n_pages), page_table.T))
    o = acc / l[..., None]
    return jnp.moveaxis(o[:, 0] - lam * o[:, 1], 1, 2)


def diff_output(o, g, sub_gain, lam_init, w_out):
    B, T = o.shape[:2]
    o = rms_norm(o, sub_gain) * (1.0 - lam_init)
    o = o.reshape(B, T, DIFF_V).astype(g.dtype)
    return (o * jax.nn.silu(g)) @ w_out


def rg_lru_branch(h, h0, conv0, w_in, conv_w, conv_b, wa, ba, wx, bx, lam_param, w_out):
    B, T, _ = h.shape
    xb, g = jnp.split(h @ w_in, 2, axis=-1)
    xc = jnp.concatenate([conv0.astype(xb.dtype), xb], axis=1)
    xconv = conv_b + sum(xc[:, kk:kk + T] * conv_w[kk] for kk in range(CONV_W))
    conv_new = xc[:, T:]
    xblk = xconv.reshape(B, T, LRU_BLOCKS, LRU_BW)
    r = jax.nn.sigmoid(jnp.einsum('btnc,ncd->btnd', xblk, wa).reshape(B, T, LRU_WIDTH) + ba)
    i = jax.nn.sigmoid(jnp.einsum('btnc,ncd->btnd', xblk, wx).reshape(B, T, LRU_WIDTH) + bx)
    log_a = -LRU_C * r.astype(jnp.float32) * jax.nn.softplus(-lam_param.astype(jnp.float32))
    a = jnp.exp(log_a)
    b = jnp.sqrt(-jnp.expm1(2.0 * log_a)) * (i * xconv).astype(jnp.float32)
    b = b.at[:, 0].add(a[:, 0] * h0.astype(jnp.float32))

    def combine(e1, e2):
        a1, b1 = e1
        a2, b2 = e2
        return a1 * a2, a2 * b1 + b2

    _, hs = lax.associative_scan(combine, (a, b), axis=1)
    out = (hs.astype(h.dtype) * jax.nn.silu(g)) @ w_out
    return out, hs[:, -1], conv_new


def setup_inputs(seed: int = 0) -> dict:
    key = jax.random.key(seed)
    ks = iter(jax.random.split(key, 48))
    nrm = lambda shape, scale: jax.random.normal(next(ks), shape, jnp.float32) * scale
    n_pages = PAST_LEN // PAGE_SIZE
    n_used = DEC_BATCH * n_pages
    n_phys = (5 * n_used + 3) // 4
    page_table = jax.random.permutation(next(ks), n_phys)[:n_used].reshape(DEC_BATCH, n_pages).astype(jnp.int32)
    u = jax.random.uniform(next(ks), (N_LRU, LRU_WIDTH), jnp.float32, minval=0.9, maxval=0.999)
    sig = u ** (1.0 / LRU_C)
    lru_lambda = jnp.log(sig) - jnp.log1p(-sig)
    return {
        'x_prompt': nrm((BATCH, SEQ, D_MODEL), 1.0),
        'x_sample': nrm((DEC_BATCH, DEC_SEQ, D_MODEL), 1.0),
        'state_ret': nrm((N_RET, DEC_BATCH, RET_HEADS, RET_DK, RET_DV), 0.5),
        'cache_k': nrm((N_DIFF, n_phys, PAGE_SIZE, DIFF_HEADS, 2 * DIFF_DK), 1.0),
        'cache_v': nrm((N_DIFF, n_phys, PAGE_SIZE, DIFF_HEADS, DIFF_DV), 1.0),
        'page_table': page_table,
        'state_lru_h': nrm((N_LRU, DEC_BATCH, LRU_WIDTH), 0.5),
        'state_lru_conv': nrm((N_LRU, DEC_BATCH, CONV_W - 1, LRU_WIDTH), 1.0),
        'norm_g': 1.0 + nrm((DEPTH, D_MODEL), 0.02),
        'ret_w_in': nrm((N_RET, D_MODEL, 2 * RET_QK + 2 * RET_V), D_MODEL ** -0.5),
        'ret_w_out': nrm((N_RET, RET_V, D_MODEL), RET_V ** -0.5),
        'diff_w_in': nrm((N_DIFF, D_MODEL, 2 * DIFF_QK + 2 * DIFF_V), D_MODEL ** -0.5),
        'diff_q_gain': 1.0 + nrm((N_DIFF, DIFF_DK), 0.02),
        'diff_k_gain': 1.0 + nrm((N_DIFF, DIFF_DK), 0.02),
        'diff_lam_q1': nrm((N_DIFF, DIFF_DK), 0.1),
        'diff_lam_k1': nrm((N_DIFF, DIFF_DK), 0.1),
        'diff_lam_q2': nrm((N_DIFF, DIFF_DK), 0.1),
        'diff_lam_k2': nrm((N_DIFF, DIFF_DK), 0.1),
        'diff_sub_gain': 1.0 + nrm((N_DIFF, DIFF_DV), 0.02),
        'diff_w_out': nrm((N_DIFF, DIFF_V, D_MODEL), DIFF_V ** -0.5),
        'lru_w_in': nrm((N_LRU, D_MODEL, 2 * LRU_WIDTH), D_MODEL ** -0.5),
        'lru_conv_w': nrm((N_LRU, CONV_W, LRU_WIDTH), CONV_W ** -0.5),
        'lru_conv_b': nrm((N_LRU, LRU_WIDTH), 0.02),
        'lru_wa': nrm((N_LRU, LRU_BLOCKS, LRU_BW, LRU_BW), LRU_BW ** -0.5),
        'lru_ba': nrm((N_LRU, LRU_WIDTH), 0.1),
        'lru_wx': nrm((N_LRU, LRU_BLOCKS, LRU_BW, LRU_BW), LRU_BW ** -0.5),
        'lru_bx': nrm((N_LRU, LRU_WIDTH), 0.1),
        'lru_lambda': lru_lambda,
        'lru_w_out': nrm((N_LRU, LRU_WIDTH, D_MODEL), LRU_WIDTH ** -0.5),
    }


def reference(x_prompt, x_sample, state_ret, cache_k, cache_v, page_table, state_lru_h, state_lru_conv,
              norm_g, ret_w_in, ret_w_out, diff_w_in, diff_q_gain, diff_k_gain,
              diff_lam_q1, diff_lam_k1, diff_lam_q2, diff_lam_k2, diff_sub_gain, diff_w_out,
              lru_w_in, lru_conv_w, lru_conv_b, lru_wa, lru_ba, lru_wx, lru_bx, lru_lambda, lru_w_out):
    xp, xs = x_prompt, x_sample
    Bp, Tp = xp.shape[:2]
    ret_p, ret_s, kp_l, vp_l, ks_l, vs_l, hp_l, cp_l, hs_l, cs_l = ([] for _ in range(10))
    for i in range(DEPTH):
        kind, j = i % N_MIXERS, i // N_MIXERS
        hp = rms_norm(xp, norm_g[i])
        hs = rms_norm(xs, norm_g[i])
        if kind == 0:
            S0 = jnp.zeros((Bp, RET_HEADS, RET_DK, RET_DV), jnp.float32)
            op, Sp = retention_branch(hp, S0, ret_w_in[j], ret_w_out[j])
            os_, Ss = retention_branch(hs, state_ret[j], ret_w_in[j], ret_w_out[j])
            ret_p.append(Sp.astype(state_ret.dtype))
            ret_s.append(Ss.astype(state_ret.dtype))
        elif kind == 1:
            lam_init = 0.8 - 0.6 * math.exp(-0.3 * i)
            lam = diff_lambda(diff_lam_q1[j], diff_lam_k1[j], diff_lam_q2[j], diff_lam_k2[j], lam_init)
            q, k, v, g = diff_project(hp, diff_w_in[j], diff_q_gain[j], diff_k_gain[j])
            op = diff_output(diff_attend_prompt(q, k, v, lam), g, diff_sub_gain[j], lam_init, diff_w_out[j])
            kp_l.append(k.reshape(Bp, Tp, DIFF_HEADS, 2 * DIFF_DK))
            vp_l.append(v)
            q, k, v, g = diff_project(hs, diff_w_in[j], diff_q_gain[j], diff_k_gain[j])
            o_s = diff_attend_sample(q, k, v, cache_k, cache_v, j, page_table, lam)
            os_ = diff_output(o_s, g, diff_sub_gain[j], lam_init, diff_w_out[j])
            ks_l.append(k.reshape(xs.shape[0], xs.shape[1], DIFF_HEADS, 2 * DIFF_DK))
            vs_l.append(v)
        else:
            lru_args = (lru_w_in[j], lru_conv_w[j], lru_conv_b[j], lru_wa[j], lru_ba[j],
                        lru_wx[j], lru_bx[j], lru_lambda[j], lru_w_out[j])
            h0 = jnp.zeros((Bp, LRU_WIDTH), jnp.float32)
            c0 = jnp.zeros((Bp, CONV_W - 1, LRU_WIDTH), hp.dtype)
            op, hpl, cpl = rg_lru_branch(hp, h0, c0, *lru_args)
            os_, hsl, csl = rg_lru_branch(hs, state_lru_h[j], state_lru_conv[j], *lru_args)
            hp_l.append(hpl.astype(state_lru_h.dtype))
            cp_l.append(cpl.astype(state_lru_conv.dtype))
            hs_l.append(hsl.astype(state_lru_h.dtype))
            cs_l.append(csl.astype(state_lru_conv.dtype))
        xp = xp + op.astype(xp.dtype)
        xs = xs + os_.astype(xs.dtype)
    return (xp, xs, jnp.stack(ret_p), jnp.stack(ret_s), jnp.stack(kp_l), jnp.stack(vp_l),
            jnp.stack(ks_l), jnp.stack(vs_l), jnp.stack(hp_l), jnp.stack(cp_l), jnp.stack(hs_l), jnp.stack(cs_l))
```

```python
import functools
import math

import jax
import jax.numpy as jnp
from jax import lax
from jax.experimental import pallas as pl
from jax.experimental.pallas import tpu as pltpu

F32 = jnp.float32
BF16 = jnp.bfloat16

D_MODEL = 1024
NORM_EPS = 1e-6

RET_HEADS = 4
RET_DK = 256
RET_DV = 512
RET_QK = RET_HEADS * RET_DK
RET_V = RET_HEADS * RET_DV
RET_CHUNK = 128

DIFF_HEADS = 8
DIFF_DK = 64
DIFF_DV = 128
DIFF_QK = DIFF_HEADS * 2 * DIFF_DK
DIFF_V = DIFF_HEADS * DIFF_DV
PAGE_SIZE = 128

LRU_WIDTH = 1536
LRU_BLOCKS = 12
LRU_BW = 128
CONV_W = 4
LRU_C = 8.0

VMEM_LIMIT_BYTES = 56 * 1024 * 1024

PROJ_TM = 1024
PROJ_TN = 512
ATTN_TQ = 256
ATTN_TK = 256
LRU_TT = 256
DEC_PAGES = 8

NT_DIMS = (((1,), (1,)), ((), ()))


def _silu(g):
    return g * jax.nn.sigmoid(g)


def _params(*sem):
    return pltpu.CompilerParams(dimension_semantics=sem, vmem_limit_bytes=VMEM_LIMIT_BYTES)


def _rms_to_bf16(x_ref, gain_ref, h_ref):
    x = x_ref[...]
    ms = jnp.mean(x * x, axis=-1, keepdims=True)
    h_ref[...] = (x * lax.rsqrt(ms + NORM_EPS) * gain_ref[...]).astype(BF16)


def _norm_proj_kernel(x_ref, gain_ref, w_ref, o_ref, h_ref):
    @pl.when(pl.program_id(1) == 0)
    def _():
        _rms_to_bf16(x_ref, gain_ref, h_ref)

    y = jnp.dot(h_ref[...], w_ref[...].astype(BF16), preferred_element_type=F32)
    o_ref[...] = y.astype(o_ref.dtype)


def norm_proj(x, gain, w, layer, *, out_dtype):
    n, d = x.shape
    f = w.shape[2]
    tm = min(PROJ_TM, n)
    return pl.pallas_call(
        _norm_proj_kernel,
        grid=(n // tm, f // PROJ_TN),
        in_specs=[
            pl.BlockSpec((tm, d), lambda i, j: (i, 0)),
            pl.BlockSpec((1, d), lambda i, j: (0, 0)),
            pl.BlockSpec((None, d, PROJ_TN), lambda i, j: (layer, 0, j)),
        ],
        out_specs=pl.BlockSpec((tm, PROJ_TN), lambda i, j: (i, j)),
        out_shape=jax.ShapeDtypeStruct((n, f), out_dtype),
        scratch_shapes=[pltpu.VMEM((tm, d), BF16)],
        compiler_params=_params("parallel", "arbitrary"),
        name="norm_proj",
    )(x, gain.reshape(1, d), w)


def _out_proj_kernel(a_ref, w_ref, x_ref, o_ref):
    y = jnp.dot(a_ref[...].astype(BF16), w_ref[...].astype(BF16), preferred_element_type=F32)
    o_ref[...] = x_ref[...] + y


def out_proj(a, w, layer, x):
    n, kf = a.shape
    d = x.shape[1]
    tm = min(PROJ_TM, n)
    return pl.pallas_call(
        _out_proj_kernel,
        grid=(n // tm, d // PROJ_TN),
        in_specs=[
            pl.BlockSpec((tm, kf), lambda i, j: (i, 0)),
            pl.BlockSpec((None, kf, PROJ_TN), lambda i, j: (layer, 0, j)),
            pl.BlockSpec((tm, PROJ_TN), lambda i, j: (i, j)),
        ],
        out_specs=pl.BlockSpec((tm, PROJ_TN), lambda i, j: (i, j)),
        out_shape=jax.ShapeDtypeStruct((n, d), F32),
        compiler_params=_params("parallel", "arbitrary"),
        name="out_proj",
    )(a, w, x)


def _ret_tables(chunk):
    lg = jnp.log1p(-jnp.exp2(-5.0 - jnp.arange(RET_HEADS, dtype=F32)))
    idx = jnp.arange(chunk, dtype=F32)
    rel = idx[:, None] - idx[None, :]
    kscale = RET_DK ** -0.5
    decay = jnp.where(rel >= 0, jnp.exp(jnp.maximum(rel, 0.0)[None] * lg[:, None, None]), 0.0) * kscale
    q_dec = jnp.exp((idx + 1.0)[None, :] * lg[:, None])[..., None]
    k_dec = jnp.exp((chunk - 1.0 - idx)[None, :] * lg[:, None])[..., None] * kscale
    s_dec = jnp.exp(chunk * lg)[:, None, None]
    return decay, q_dec, k_dec, s_dec


def _head_norm_gate(o, g):
    ms = jnp.mean(o * o, axis=-1, keepdims=True)
    return o * lax.rsqrt(ms + NORM_EPS) * _silu(g)


def _ret_prompt_kernel(q_ref, k_ref, v_ref, g_ref, dec_ref, qd_ref, kd_ref, sd_ref, og_ref, s_ref):
    @pl.when(pl.program_id(1) == 0)
    def _():
        s_ref[...] = jnp.zeros_like(s_ref)

    for h in range(RET_HEADS):
        qk_cols = slice(h * RET_DK, (h + 1) * RET_DK)
        v_cols = slice(h * RET_DV, (h + 1) * RET_DV)
        q = q_ref[:, qk_cols]
        k = k_ref[:, qk_cols]
        v = v_ref[:, v_cols]
        s = lax.dot_general(q, k, NT_DIMS, preferred_element_type=F32) * dec_ref[h]
        o = jnp.dot(s.astype(BF16), v, preferred_element_type=F32)
        state = s_ref[0, h]
        o = o + jnp.dot(q, state.astype(BF16), preferred_element_type=F32) * qd_ref[h]
        kd_t = (k.astype(F32) * kd_ref[h]).T.astype(BF16)
        s_ref[0, h] = sd_ref[h] * state + jnp.dot(kd_t, v, preferred_element_type=F32)
        og_ref[:, v_cols] = _head_norm_gate(o, g_ref[:, v_cols].astype(F32)).astype(og_ref.dtype)


def ret_prompt(proj, batch, seq):
    chunk = RET_CHUNK
    nc = seq // chunk
    decay, q_dec, k_dec, s_dec = _ret_tables(chunk)
    row = lambda b, c: b * nc + c
    const3 = lambda b, c: (0, 0, 0)
    return pl.pallas_call(
        _ret_prompt_kernel,
        grid=(batch, nc),
        in_specs=[
            pl.BlockSpec((chunk, RET_QK), lambda b, c: (row(b, c), 0)),
            pl.BlockSpec((chunk, RET_QK), lambda b, c: (row(b, c), 1)),
            pl.BlockSpec((chunk, RET_V), lambda b, c: (row(b, c), 1)),
            pl.BlockSpec((chunk, RET_V), lambda b, c: (row(b, c), 2)),
            pl.BlockSpec((RET_HEADS, chunk, chunk), const3),
            pl.BlockSpec((RET_HEADS, chunk, 1), const3),
            pl.BlockSpec((RET_HEADS, chunk, 1), const3),
            pl.BlockSpec((RET_HEADS, 1, 1), const3),
        ],
        out_specs=[
            pl.BlockSpec((chunk, RET_V), lambda b, c: (row(b, c), 0)),
            pl.BlockSpec((1, RET_HEADS, RET_DK, RET_DV), lambda b, c: (b, 0, 0, 0)),
        ],
        out_shape=[
            jax.ShapeDtypeStruct((batch * seq, RET_V), BF16),
            jax.ShapeDtypeStruct((batch, RET_HEADS, RET_DK, RET_DV), F32),
        ],
        compiler_params=_params("parallel", "arbitrary"),
        name="ret_prompt",
    )(proj, proj, proj, proj, decay, q_dec, k_dec, s_dec)


def _ret_sample_kernel(qc_ref, kc_ref, v_ref, g_ref, s0_ref, qd_ref, sd_ref, og_ref, s_ref):
    for h in range(RET_HEADS):
        qc = qc_ref[0, h]
        kc = kc_ref[0, h]
        v = v_ref[0, h]
        s0 = s0_ref[0, h]
        qk = jnp.sum(qc * kc, axis=0, keepdims=True)
        o = qk * v + jnp.sum(qc * s0, axis=0, keepdims=True) * qd_ref[h]
        s_ref[0, h] = sd_ref[h] * s0 + kc * v
        og_ref[0, h] = _head_norm_gate(o, g_ref[0, h])


def ret_sample(proj, state, layer):
    n = proj.shape[0]
    _, q_dec, k_dec, s_dec = _ret_tables(1)
    q, k, v, g = jnp.split(proj, [RET_QK, 2 * RET_QK, 2 * RET_QK + RET_V], axis=-1)
    qc = q.reshape(n, RET_HEADS, RET_DK, 1)
    kc = (k * k_dec[0, 0, 0]).reshape(n, RET_HEADS, RET_DK, 1)
    v = v.reshape(n, RET_HEADS, 1, RET_DV)
    g = g.reshape(n, RET_HEADS, 1, RET_DV)
    col_spec = pl.BlockSpec((1, RET_HEADS, RET_DK, 1), lambda b: (b, 0, 0, 0))
    row_spec = pl.BlockSpec((1, RET_HEADS, 1, RET_DV), lambda b: (b, 0, 0, 0))
    dec_spec = pl.BlockSpec((RET_HEADS, 1, 1), lambda b: (0, 0, 0))
    og, s_new = pl.pallas_call(
        _ret_sample_kernel,
        grid=(n,),
        in_specs=[
            col_spec,
            col_spec,
            row_spec,
            row_spec,
            pl.BlockSpec((None, 1, RET_HEADS, RET_DK, RET_DV), lambda b: (layer, b, 0, 0, 0)),
            dec_spec,
            dec_spec,
        ],
        out_specs=[row_spec, pl.BlockSpec((1, RET_HEADS, RET_DK, RET_DV), lambda b: (b, 0, 0, 0))],
        out_shape=[
            jax.ShapeDtypeStruct((n, RET_HEADS, 1, RET_DV), F32),
            jax.ShapeDtypeStruct((n, RET_HEADS, RET_DK, RET_DV), F32),
        ],
        compiler_params=_params("parallel"),
        name="ret_sample",
    )(qc, kc, v, g, state, q_dec, s_dec)
    return og.reshape(n, RET_V), s_new


def _alibi_slopes():
    return jnp.exp2(-8.0 * (jnp.arange(DIFF_HEADS, dtype=F32) + 1.0) / DIFF_HEADS)


def _group_mean_matrix(width):
    gid = jnp.arange(width) // DIFF_DK
    return (jnp.where(gid[:, None] == gid[None, :], 1.0 / DIFF_DK, 0.0)).astype(BF16)


def _diff_proj_kernel(x_ref, gain_ref, w_ref, gm_ref, qg_ref, kg_ref, qk_ref, kf_ref, vf_ref, vt_ref, g_ref, h_ref,
                      *, emit_vt):
    j = pl.program_id(1)

    @pl.when(j == 0)
    def _():
        _rms_to_bf16(x_ref, gain_ref, h_ref)

    y = jnp.dot(h_ref[...], w_ref[...].astype(BF16), preferred_element_type=F32)
    nq = DIFF_QK // PROJ_TN
    nv = DIFF_V // PROJ_TN

    def qk_norm(gain):
        ms = jnp.dot((y * y).astype(BF16), gm_ref[...], preferred_element_type=F32)
        return y * lax.rsqrt(ms + NORM_EPS) * gain

    @pl.when(j < nq)
    def _():
        qk_ref[...] = (qk_norm(qg_ref[...]) * (DIFF_DK ** -0.5)).astype(BF16)

    @pl.when((j >= nq) & (j < 2 * nq))
    def _():
        kn = qk_norm(kg_ref[...])
        kf_ref[...] = kn
        qk_ref[...] = kn.astype(BF16)

    @pl.when((j >= 2 * nq) & (j < 2 * nq + nv))
    def _():
        vf_ref[...] = y
        if emit_vt:
            vt_ref[...] = y.T.astype(BF16)
        else:
            vt_ref[...] = y.astype(BF16)

    @pl.when(j >= 2 * nq + nv)
    def _():
        g_ref[...] = y.astype(g_ref.dtype)


def diff_proj(x, gain, w, layer, q_gain, k_gain, *, emit_vt, g_dtype):
    n, d = x.shape
    tm = min(PROJ_TM, n)
    tn = PROJ_TN
    nq = DIFF_QK // tn
    nv = DIFF_V // tn
    f = 2 * DIFF_QK + 2 * DIFF_V
    clamp = lambda j, lo, cnt: jnp.clip(j - lo, 0, cnt - 1)
    tile_gain = lambda a: jnp.tile(a, tn // DIFF_DK).reshape(1, tn)
    if emit_vt:
        vt_spec = pl.BlockSpec((tn, tm), lambda i, j: (clamp(j, 2 * nq, nv), i))
        vt_shape = jax.ShapeDtypeStruct((DIFF_V, n), BF16)
    else:
        vt_spec = pl.BlockSpec((tm, tn), lambda i, j: (i, clamp(j, 2 * nq, nv)))
        vt_shape = jax.ShapeDtypeStruct((n, DIFF_V), BF16)
    return pl.pallas_call(
        functools.partial(_diff_proj_kernel, emit_vt=emit_vt),
        grid=(n // tm, f // tn),
        in_specs=[
            pl.BlockSpec((tm, d), lambda i, j: (i, 0)),
            pl.BlockSpec((1, d), lambda i, j: (0, 0)),
            pl.BlockSpec((None, d, tn), lambda i, j: (layer, 0, j)),
            pl.BlockSpec((tn, tn), lambda i, j: (0, 0)),
            pl.BlockSpec((1, tn), lambda i, j: (0, 0)),
            pl.BlockSpec((1, tn), lambda i, j: (0, 0)),
        ],
        out_specs=[
            pl.BlockSpec((tm, tn), lambda i, j: (i, clamp(j, 0, 2 * nq))),
            pl.BlockSpec((tm, tn), lambda i, j: (i, clamp(j, nq, nq))),
            pl.BlockSpec((tm, tn), lambda i, j: (i, clamp(j, 2 * nq, nv))),
            vt_spec,
            pl.BlockSpec((tm, tn), lambda i, j: (i, clamp(j, 2 * nq + nv, nv))),
        ],
        out_shape=[
            jax.ShapeDtypeStruct((n, 2 * DIFF_QK), BF16),
            jax.ShapeDtypeStruct((n, DIFF_QK), F32),
            jax.ShapeDtypeStruct((n, DIFF_V), F32),
            vt_shape,
            jax.ShapeDtypeStruct((n, DIFF_V), g_dtype),
        ],
        scratch_shapes=[pltpu.VMEM((tm, d), BF16)],
        compiler_params=_params("parallel", "arbitrary"),
        name="diff_proj",
    )(x, gain.reshape(1, d), w, _group_mean_matrix(tn), tile_gain(q_gain), tile_gain(k_gain))


def _diff_lambda(lq1_ref, lk1_ref, lq2_ref, lk2_ref, lam_init):
    l1 = jnp.sum(lq1_ref[...] * lk1_ref[...], axis=-1, keepdims=True)
    l2 = jnp.sum(lq2_ref[...] * lk2_ref[...], axis=-1, keepdims=True)
    return jnp.exp(l1) - jnp.exp(l2) + lam_init


def _attn_kernel(slope_ref, q_ref, k_ref, vt_ref, g_ref, lq1_ref, lk1_ref, lq2_ref, lk2_ref, subg_ref, og_ref,
                 q2_ref, m_ref, l_ref, acc_ref, *, lam_init):
    tq, tk = ATTN_TQ, ATTN_TK
    qi = pl.program_id(2)
    slope = slope_ref[pl.program_id(1)]

    lane = lax.broadcasted_iota(jnp.int32, (tq, 2 * DIFF_DK), 1)
    q = q_ref[...]
    zero = jnp.zeros_like(q)
    ones_feat = jnp.where(lane == 0, 1.0, 0.0).astype(BF16)
    q2_ref[0:tq, 0:128] = jnp.where(lane < DIFF_DK, q, zero)
    q2_ref[tq:2 * tq, 0:128] = jnp.where(lane >= DIFF_DK, q, zero)
    q2_ref[0:tq, 128:256] = ones_feat
    q2_ref[tq:2 * tq, 128:256] = ones_feat

    key_lane = lax.broadcasted_iota(jnp.int32, (tk, 2 * DIFF_DK), 1)
    key_row = lax.broadcasted_iota(jnp.int32, (tk, 2 * DIFF_DK), 0).astype(F32)
    key_feat = jnp.where(key_lane == 0, key_row * slope, 0.0).astype(BF16)

    m_ref[...] = jnp.full_like(m_ref, -1e30)
    l_ref[...] = jnp.zeros_like(l_ref)
    acc_ref[...] = jnp.zeros_like(acc_ref)

    def block(ki, masked):
        start = pl.multiple_of(ki * tk, tk)
        k_aug = jnp.concatenate([k_ref[pl.ds(start, tk), :], key_feat], axis=1)
        s = lax.dot_general(k_aug, q2_ref[...], NT_DIMS, preferred_element_type=F32)
        if masked:
            key_pos = lax.broadcasted_iota(jnp.int32, (tk, 2 * tq), 0)
            col = lax.broadcasted_iota(jnp.int32, (tk, 2 * tq), 1)
            q_pos = jnp.where(col >= tq, col - tq, col)
            s = jnp.where(key_pos <= q_pos, s, -1e30)
        base = slope * (ki * tk).astype(F32)
        m_old = m_ref[...]
        m_new = jnp.maximum(m_old, jnp.max(s, axis=0, keepdims=True) + base)
        p = jnp.exp(s + (base - m_new))
        corr = jnp.exp(m_old - m_new)
        l_ref[...] = l_ref[...] * corr + jnp.sum(p, axis=0, keepdims=True)
        pv = jnp.dot(vt_ref[:, pl.ds(start, tk)], p.astype(BF16), preferred_element_type=F32)
        acc_ref[...] = acc_ref[...] * corr + pv
        m_ref[...] = m_new

    def body(ki, carry):
        block(ki, False)
        return carry

    lax.fori_loop(0, qi, body, 0)
    block(qi, True)

    lam = _diff_lambda(lq1_ref, lk1_ref, lq2_ref, lk2_ref, lam_init)
    o_t = acc_ref[...] / l_ref[...]
    o_t = o_t[:, 0:tq] - lam * o_t[:, tq:2 * tq]
    ms = jnp.mean(o_t * o_t, axis=0, keepdims=True)
    o_t = o_t * lax.rsqrt(ms + NORM_EPS) * subg_ref[...] * (1.0 - lam_init)
    og_ref[...] = (o_t.T * _silu(g_ref[...].astype(F32))).astype(og_ref.dtype)


def diff_attn_prompt(qk, vt, g, lam_vecs, sub_gain, batch, seq, lam_init):
    tq = ATTN_TQ
    nq = seq // tq
    row = lambda b, h, i: b * nq + i
    vec_spec = pl.BlockSpec((1, DIFF_DK), lambda b, h, i: (0, 0))
    return pl.pallas_call(
        functools.partial(_attn_kernel, lam_init=lam_init),
        grid=(batch, DIFF_HEADS, nq),
        in_specs=[
            pl.BlockSpec(memory_space=pltpu.SMEM),
            pl.BlockSpec((tq, DIFF_DV), lambda b, h, i: (row(b, h, i), h)),
            pl.BlockSpec((seq, DIFF_DV), lambda b, h, i: (b, DIFF_HEADS + h)),
            pl.BlockSpec((DIFF_DV, seq), lambda b, h, i: (h, b)),
            pl.BlockSpec((tq, DIFF_DV), lambda b, h, i: (row(b, h, i), h)),
            vec_spec,
            vec_spec,
            vec_spec,
            vec_spec,
            pl.BlockSpec((DIFF_DV, 1), lambda b, h, i: (0, 0)),
        ],
        out_specs=pl.BlockSpec((tq, DIFF_DV), lambda b, h, i: (row(b, h, i), h)),
        out_shape=jax.ShapeDtypeStruct((batch * seq, DIFF_V), BF16),
        scratch_shapes=[
            pltpu.VMEM((2 * tq, 256), BF16),
            pltpu.VMEM((1, 2 * tq), F32),
            pltpu.VMEM((1, 2 * tq), F32),
            pltpu.VMEM((DIFF_DV, 2 * tq), F32),
        ],
        compiler_params=_params("parallel", "parallel", "arbitrary"),
        name="diff_attn_prompt",
    )(_alibi_slopes(), qk, qk, vt, g, *lam_vecs, sub_gain.reshape(DIFF_DV, 1))


def _row_to_col(row):
    return jnp.broadcast_to(row, (128, 128)).T


def _decode_kernel(pt_ref, qc_ref, kn_ref, vn_ref, g_ref, slope_ref, lq1_ref, lk1_ref, lq2_ref, lk2_ref, subg_ref,
                   *rest, past, lam_init):
    np_ = DEC_PAGES
    k_refs = rest[:np_]
    v_refs = rest[np_:2 * np_]
    og_ref, qbd_ref, sc_ref, m_ref, l_ref, acc_ref, self_ref = rest[2 * np_:]
    t = pl.program_id(1)
    n_steps = past // PAGE_SIZE // np_

    @pl.when(t == 0)
    def _():
        row = lax.broadcasted_iota(jnp.int32, (DIFF_QK, 128), 0)
        col = lax.broadcasted_iota(jnp.int32, (DIFF_QK, 128), 1)
        qbd = jnp.where(row // DIFF_DK == col, qc_ref[0], 0.0).astype(BF16)
        qbd_ref[...] = qbd
        k_new = jnp.broadcast_to(kn_ref[0], (8, DIFF_QK)).astype(BF16)
        s_self = jnp.dot(k_new, qbd, preferred_element_type=F32)[0:1]
        self_ref[...] = s_self
        m_ref[...] = s_self
        l_ref[...] = jnp.zeros_like(l_ref)
        acc_ref[...] = jnp.zeros_like(acc_ref)

    @pl.when(t < n_steps)
    def _():
        pos = lax.broadcasted_iota(jnp.int32, (PAGE_SIZE, 128), 0).astype(F32)
        m = m_ref[...]
        for p in range(np_):
            page = t * np_ + p
            s = jnp.dot(k_refs[p][0].astype(BF16), qbd_ref[...], preferred_element_type=F32)
            dist = past - (page * PAGE_SIZE).astype(F32) - pos
            s = s - slope_ref[...] * dist
            sc_ref[pl.ds(pl.multiple_of(page * PAGE_SIZE, PAGE_SIZE), PAGE_SIZE), :] = s
            m = jnp.maximum(m, jnp.max(s, axis=0, keepdims=True))
        m_ref[...] = m

    @pl.when(t >= n_steps)
    def _():
        m = m_ref[...]
        l = l_ref[...]
        acc = acc_ref[...]
        for p in range(np_):
            page = (t - n_steps) * np_ + p
            s = sc_ref[pl.ds(pl.multiple_of(page * PAGE_SIZE, PAGE_SIZE), PAGE_SIZE), :]
            e = jnp.exp(s - m)
            l = l + jnp.sum(e, axis=0, keepdims=True)
            acc = acc + jnp.dot(e.T.astype(BF16), v_refs[p][0].astype(BF16), preferred_element_type=F32)
        l_ref[...] = l
        acc_ref[...] = acc

    @pl.when(t == 2 * n_steps - 1)
    def _():
        e_self = jnp.exp(self_ref[...] - m_ref[...])
        inv_l = _row_to_col(1.0 / (l_ref[...] + e_self))
        e_self = _row_to_col(e_self)
        lam = _diff_lambda(lq1_ref, lk1_ref, lq2_ref, lk2_ref, lam_init)
        for h in range(DIFF_HEADS):
            cols = slice(h * DIFF_DV, (h + 1) * DIFF_DV)
            v_new = vn_ref[0, :, cols]
            o = []
            for mp in range(2):
                r = 2 * h + mp
                a = acc_ref[r:r + 1, cols] + e_self[r:r + 1, :] * v_new
                o.append(a * inv_l[r:r + 1, :])
            od = o[0] - lam * o[1]
            ms = jnp.mean(od * od, axis=-1, keepdims=True)
            od = od * lax.rsqrt(ms + NORM_EPS) * subg_ref[...] * (1.0 - lam_init)
            og_ref[0, :, cols] = od * _silu(g_ref[0, :, cols])


def diff_attn_sample(qk, k_new, v_new, g, cache_k, cache_v, layer, page_table, lam_vecs, sub_gain, lam_init):
    n, n_pages = page_table.shape
    past = n_pages * PAGE_SIZE
    n_phys = cache_k.shape[1]
    n_steps = n_pages // DEC_PAGES
    ck = cache_k.reshape(cache_k.shape[0] * n_phys, PAGE_SIZE, DIFF_QK)
    cv = cache_v.reshape(cache_v.shape[0] * n_phys, PAGE_SIZE, DIFF_V)
    pt = page_table + layer * n_phys
    qc = qk[:, :DIFF_QK].astype(F32).reshape(n, DIFF_QK, 1)
    lane = jnp.arange(128)
    slopes = jnp.where(lane < 2 * DIFF_HEADS, _alibi_slopes()[jnp.minimum(lane // 2, DIFF_HEADS - 1)], 0.0)
    tok_spec = pl.BlockSpec((1, 1, DIFF_QK), lambda b, t, pt: (b, 0, 0))
    vec_spec = pl.BlockSpec((1, DIFF_DK), lambda b, t, pt: (0, 0))

    def k_spec(p):
        return pl.BlockSpec((1, PAGE_SIZE, DIFF_QK),
                            lambda b, t, pt: (pt[b, jnp.minimum(t, n_steps - 1) * DEC_PAGES + p], 0, 0))

    def v_spec(p):
        return pl.BlockSpec((1, PAGE_SIZE, DIFF_V),
                            lambda b, t, pt: (pt[b, jnp.maximum(t - n_steps, 0) * DEC_PAGES + p], 0, 0))

    grid_spec = pltpu.PrefetchScalarGridSpec(
        num_scalar_prefetch=1,
        grid=(n, 2 * n_steps),
        in_specs=[
            pl.BlockSpec((1, DIFF_QK, 1), lambda b, t, pt: (b, 0, 0)),
            tok_spec,
            tok_spec,
            tok_spec,
            pl.BlockSpec((1, 128), lambda b, t, pt: (0, 0)),
            vec_spec,
            vec_spec,
            vec_spec,
            vec_spec,
            pl.BlockSpec((1, DIFF_DV), lambda b, t, pt: (0, 0)),
        ] + [k_spec(p) for p in range(DEC_PAGES)] + [v_spec(p) for p in range(DEC_PAGES)],
        out_specs=tok_spec,
        scratch_shapes=[
            pltpu.VMEM((DIFF_QK, 128), BF16),
            pltpu.VMEM((past, 128), F32),
            pltpu.VMEM((1, 128), F32),
            pltpu.VMEM((1, 128), F32),
            pltpu.VMEM((128, DIFF_V), F32),
            pltpu.VMEM((1, 128), F32),
        ],
    )
    og = pl.pallas_call(
        functools.partial(_decode_kernel, past=past, lam_init=lam_init),
        grid_spec=grid_spec,
        out_shape=jax.ShapeDtypeStruct((n, 1, DIFF_V), F32),
        compiler_params=_params("parallel", "arbitrary"),
        name="diff_attn_sample",
    )(pt, qc, k_new.reshape(n, 1, DIFF_QK), v_new.reshape(n, 1, DIFF_V), g.reshape(n, 1, DIFF_V),
      slopes.reshape(1, 128), *lam_vecs, sub_gain.reshape(1, DIFF_DV), *([ck] * DEC_PAGES), *([cv] * DEC_PAGES))
    return og.reshape(n, DIFF_V)


def _softplus(x):
    return jnp.maximum(x, 0.0) + jnp.log1p(jnp.exp(-jnp.abs(x)))


def _lru_gates(xconv, wcat_ref, ba_ref, bx_ref, lam_ref, a_ref, b_ref):
    for n in range(LRU_BLOCKS):
        cols = slice(n * LRU_BW, (n + 1) * LRU_BW)
        xc = xconv[:, cols]
        y = jnp.dot(xc.astype(BF16), wcat_ref[n], preferred_element_type=F32)
        r = jax.nn.sigmoid(y[:, 0:LRU_BW] + ba_ref[:, cols])
        i = jax.nn.sigmoid(y[:, LRU_BW:2 * LRU_BW] + bx_ref[:, cols])
        log_a = -LRU_C * r * _softplus(-lam_ref[:, cols])
        a = jnp.exp(log_a)
        b = jnp.sqrt(-jnp.tanh(log_a) * (a * a + 1.0)) * (i * xc)
        a_ref[:, cols] = a
        b_ref[:, cols] = b


def _lru_prompt_kernel(xb_ref, g_ref, cw_ref, cb_ref, wcat_ref, ba_ref, bx_ref, lam_ref,
                       og_ref, hlast_ref, conv_ref, xpad_ref, a_ref, b_ref, h_ref):
    tt = LRU_TT
    t = pl.program_id(1)

    @pl.when(t == 0)
    def _():
        xpad_ref[0:8, :] = jnp.zeros((8, LRU_WIDTH), F32)
        h_ref[...] = jnp.zeros_like(h_ref)

    xpad_ref[8:8 + tt, :] = xb_ref[...].astype(F32)
    xconv = cb_ref[...]
    for kk in range(CONV_W):
        off = 8 - (CONV_W - 1) + kk
        xconv = xconv + xpad_ref[off:off + tt, :] * cw_ref[kk:kk + 1, :]
    _lru_gates(xconv, wcat_ref, ba_ref, bx_ref, lam_ref, a_ref, b_ref)

    def step(r, h):
        h = a_ref[pl.ds(r, 1), :] * h + b_ref[pl.ds(r, 1), :]
        b_ref[pl.ds(r, 1), :] = h
        return h

    h = lax.fori_loop(0, tt, step, h_ref[...], unroll=8)
    h_ref[...] = h
    og_ref[...] = (b_ref[...] * _silu(g_ref[...].astype(F32))).astype(og_ref.dtype)
    xpad_ref[0:8, :] = xpad_ref[tt:tt + 8, :]

    @pl.when(t == pl.num_programs(1) - 1)
    def _():
        hlast_ref[0] = h
        conv_ref[0] = xpad_ref[tt + 8 - (CONV_W - 1):tt + 8, :]


def _lru_wcat(wa, wx, layer):
    return jnp.concatenate([wa[layer], wx[layer]], axis=-1).astype(BF16)


def lru_prompt(proj, batch, seq, conv_w, conv_b, wcat, ba, bx, lam):
    tt = LRU_TT
    nt = seq // tt
    w = LRU_WIDTH
    row = lambda b, t: b * nt + t
    vec = pl.BlockSpec((1, w), lambda b, t: (0, 0))
    og, h_last, conv_new = pl.pallas_call(
        _lru_prompt_kernel,
        grid=(batch, nt),
        in_specs=[
            pl.BlockSpec((tt, w), lambda b, t: (row(b, t), 0)),
            pl.BlockSpec((tt, w), lambda b, t: (row(b, t), 1)),
            pl.BlockSpec((CONV_W, w), lambda b, t: (0, 0)),
            vec,
            pl.BlockSpec((LRU_BLOCKS, LRU_BW, 2 * LRU_BW), lambda b, t: (0, 0, 0)),
            vec,
            vec,
            vec,
        ],
        out_specs=[
            pl.BlockSpec((tt, w), lambda b, t: (row(b, t), 0)),
            pl.BlockSpec((1, 1, w), lambda b, t: (b, 0, 0)),
            pl.BlockSpec((1, CONV_W - 1, w), lambda b, t: (b, 0, 0)),
        ],
        out_shape=[
            jax.ShapeDtypeStruct((batch * seq, w), BF16),
            jax.ShapeDtypeStruct((batch, 1, w), F32),
            jax.ShapeDtypeStruct((batch, CONV_W - 1, w), F32),
        ],
        scratch_shapes=[
            pltpu.VMEM((tt + 8, w), F32),
            pltpu.VMEM((tt, w), F32),
            pltpu.VMEM((tt, w), F32),
            pltpu.VMEM((1, w), F32),
        ],
        compiler_params=_params("parallel", "arbitrary"),
        name="lru_prompt",
    )(proj, proj, conv_w, conv_b.reshape(1, w), wcat, ba.reshape(1, w), bx.reshape(1, w), lam.reshape(1, w))
    return og, h_last.reshape(batch, w), conv_new


def _lru_sample_kernel(xb_ref, g_ref, h0_ref, c0_ref, cw_ref, cb_ref, wcat_ref, ba_ref, bx_ref, lam_ref,
                       og_ref, h_ref, conv_ref, a_ref, b_ref):
    xb = xb_ref[...]
    xconv = cb_ref[...] + xb * cw_ref[CONV_W - 1:CONV_W, :]
    for kk in range(CONV_W - 1):
        xconv = xconv + c0_ref[kk] * cw_ref[kk:kk + 1, :]
    _lru_gates(xconv, wcat_ref, ba_ref, bx_ref, lam_ref, a_ref, b_ref)
    h = a_ref[...] * h0_ref[...] + b_ref[...]
    h_ref[...] = h
    og_ref[...] = h * _silu(g_ref[...])
    for kk in range(CONV_W - 2):
        conv_ref[kk] = c0_ref[kk + 1]
    conv_ref[CONV_W - 2] = xb


def lru_sample(proj, h0, conv0, conv_w, conv_b, wcat, ba, bx, lam):
    n = proj.shape[0]
    w = LRU_WIDTH
    full = lambda shape: pl.BlockSpec(shape, lambda i: (0,) * len(shape))
    og, h, conv_new = pl.pallas_call(
        _lru_sample_kernel,
        grid=(1,),
        in_specs=[
            pl.BlockSpec((n, w), lambda i: (0, 0)),
            pl.BlockSpec((n, w), lambda i: (0, 1)),
            full((n, w)),
            full((CONV_W - 1, n, w)),
            full((CONV_W, w)),
            full((1, w)),
            full((LRU_BLOCKS, LRU_BW, 2 * LRU_BW)),
            full((1, w)),
            full((1, w)),
            full((1, w)),
        ],
        out_specs=[full((n, w)), full((n, w)), full((CONV_W - 1, n, w))],
        out_shape=[
            jax.ShapeDtypeStruct((n, w), F32),
            jax.ShapeDtypeStruct((n, w), F32),
            jax.ShapeDtypeStruct((CONV_W - 1, n, w), F32),
        ],
        scratch_shapes=[pltpu.VMEM((n, w), F32), pltpu.VMEM((n, w), F32)],
        compiler_params=_params("arbitrary"),
        name="lru_sample",
    )(proj, proj, h0, jnp.swapaxes(conv0, 0, 1), conv_w, conv_b.reshape(1, w), wcat,
      ba.reshape(1, w), bx.reshape(1, w), lam.reshape(1, w))
    return og, h, jnp.swapaxes(conv_new, 0, 1)


def kernel(x_prompt, x_sample, state_ret, cache_k, cache_v, page_table, state_lru_h, state_lru_conv, norm_g,
           ret_w_in, ret_w_out, diff_w_in, diff_q_gain, diff_k_gain, diff_lam_q1, diff_lam_k1, diff_lam_q2,
           diff_lam_k2, diff_sub_gain, diff_w_out, lru_w_in, lru_conv_w, lru_conv_b, lru_wa, lru_ba, lru_wx,
           lru_bx, lru_lambda, lru_w_out):
    bp, tp, d = x_prompt.shape
    bs = x_sample.shape[0]
    assert x_sample.shape[1] == 1
    xp = x_prompt.reshape(bp * tp, d)
    xs = x_sample.reshape(bs, d)
    depth = norm_g.shape[0]
    ret_p, ret_s, kp_l, vp_l, ks_l, vs_l, hp_l, cp_l, hs_l, cs_l = ([] for _ in range(10))
    for i in range(depth):
        kind, j = i % 3, i // 3
        if kind == 0:
            proj_p = norm_proj(xp, norm_g[i], ret_w_in, j, out_dtype=BF16)
            og_p, s_p = ret_prompt(proj_p, bp, tp)
            xp = out_proj(og_p, ret_w_out, j, xp)
            proj_s = norm_proj(xs, norm_g[i], ret_w_in, j, out_dtype=F32)
            og_s, s_s = ret_sample(proj_s, state_ret, j)
            xs = out_proj(og_s, ret_w_out, j, xs)
            ret_p.append(s_p)
            ret_s.append(s_s)
        elif kind == 1:
            lam_init = 0.8 - 0.6 * math.exp(-0.3 * i)
            lam_vecs = [a[j].reshape(1, DIFF_DK) for a in (diff_lam_q1, diff_lam_k1, diff_lam_q2, diff_lam_k2)]
            qk, k_f, v_f, vt, g = diff_proj(xp, norm_g[i], diff_w_in, j, diff_q_gain[j], diff_k_gain[j],
                                            emit_vt=True, g_dtype=BF16)
            og_p = diff_attn_prompt(qk, vt, g, lam_vecs, diff_sub_gain[j], bp, tp, lam_init)
            xp = out_proj(og_p, diff_w_out, j, xp)
            kp_l.append(k_f.reshape(bp, tp, DIFF_HEADS, 2 * DIFF_DK))
            vp_l.append(v_f.reshape(bp, tp, DIFF_HEADS, DIFF_DV))
            qk_s, k_s, v_s, _, g_s = diff_proj(xs, norm_g[i], diff_w_in, j, diff_q_gain[j], diff_k_gain[j],
                                               emit_vt=False, g_dtype=F32)
            og_s = diff_attn_sample(qk_s, k_s, v_s, g_s, cache_k, cache_v, j, page_table, lam_vecs,
                                    diff_sub_gain[j], lam_init)
            xs = out_proj(og_s, diff_w_out, j, xs)
            ks_l.append(k_s.reshape(bs, 1, DIFF_HEADS, 2 * DIFF_DK))
            vs_l.append(v_s.reshape(bs, 1, DIFF_HEADS, DIFF_DV))
        else:
            wcat = _lru_wcat(lru_wa, lru_wx, j)
            lru_args = (lru_conv_w[j], lru_conv_b[j], wcat, lru_ba[j], lru_bx[j], lru_lambda[j])
            proj_p = norm_proj(xp, norm_g[i], lru_w_in, j, out_dtype=BF16)
            og_p, h_p, c_p = lru_prompt(proj_p, bp, tp, *lru_args)
            xp = out_proj(og_p, lru_w_out, j, xp)
            proj_s = norm_proj(xs, norm_g[i], lru_w_in, j, out_dtype=F32)
            og_s, h_s, c_s = lru_sample(proj_s, state_lru_h[j], state_lru_conv[j], *lru_args)
            xs = out_proj(og_s, lru_w_out, j, xs)
            hp_l.append(h_p)
            cp_l.append(c_p)
            hs_l.append(h_s)
            cs_l.append(c_s)
    return (xp.reshape(bp, tp, d), xs.reshape(bs, 1, d), jnp.stack(ret_p), jnp.stack(ret_s), jnp.stack(kp_l),
            jnp.stack(vp_l), jnp.stack(ks_l), jnp.stack(vs_l), jnp.stack(hp_l), jnp.stack(cp_l), jnp.stack(hs_l),
            jnp.stack(cs_l))
```

```python
import functools
import math

import jax
import jax.numpy as jnp
from jax import lax
from jax.experimental import pallas as pl
from jax.experimental.pallas import tpu as pltpu

F32 = jnp.float32
BF16 = jnp.bfloat16

D_MODEL = 1024
NORM_EPS = 1e-6

RET_HEADS = 4
RET_DK = 256
RET_DV = 512
RET_QK = RET_HEADS * RET_DK
RET_V = RET_HEADS * RET_DV
RET_CHUNK = 128

DIFF_HEADS = 8
DIFF_DK = 64
DIFF_DV = 128
DIFF_QK = DIFF_HEADS * 2 * DIFF_DK
DIFF_V = DIFF_HEADS * DIFF_DV
PAGE_SIZE = 128

LRU_WIDTH = 1536
LRU_BLOCKS = 12
LRU_BW = 128
CONV_W = 4
LRU_C = 8.0

VMEM_LIMIT_BYTES = 56 * 1024 * 1024

PROJ_TM = 1024
PROJ_TN = 512
ATTN_TQ = 512
ATTN_TK = 512
LRU_TT = 256
DEC_PAGES = 8

NT_DIMS = (((1,), (1,)), ((), ()))
LOG2E = math.log2(math.e)


def _silu(g):
    return g * jax.nn.sigmoid(g)


def _params(*sem):
    return pltpu.CompilerParams(dimension_semantics=sem, vmem_limit_bytes=VMEM_LIMIT_BYTES)


def _rms_to_bf16(x_ref, gain_ref, h_ref):
    x = x_ref[...]
    ms = jnp.mean(x * x, axis=-1, keepdims=True)
    h_ref[...] = (x * lax.rsqrt(ms + NORM_EPS) * gain_ref[...]).astype(BF16)


def _norm_proj_kernel(x_ref, gain_ref, w_ref, o_ref, h_ref):
    @pl.when(pl.program_id(1) == 0)
    def _():
        _rms_to_bf16(x_ref, gain_ref, h_ref)

    y = jnp.dot(h_ref[...], w_ref[...].astype(BF16), preferred_element_type=F32)
    o_ref[...] = y.astype(o_ref.dtype)


def norm_proj(x, gain, w, layer, *, out_dtype):
    n, d = x.shape
    f = w.shape[2]
    tm = min(PROJ_TM, n)
    return pl.pallas_call(
        _norm_proj_kernel,
        grid=(n // tm, f // PROJ_TN),
        in_specs=[
            pl.BlockSpec((tm, d), lambda i, j: (i, 0)),
            pl.BlockSpec((1, d), lambda i, j: (0, 0)),
            pl.BlockSpec((None, d, PROJ_TN), lambda i, j: (layer, 0, j)),
        ],
        out_specs=pl.BlockSpec((tm, PROJ_TN), lambda i, j: (i, j)),
        out_shape=jax.ShapeDtypeStruct((n, f), out_dtype),
        scratch_shapes=[pltpu.VMEM((tm, d), BF16)],
        compiler_params=_params("parallel", "arbitrary"),
        name="norm_proj",
    )(x, gain.reshape(1, d), w)


def _out_proj_kernel(a_ref, w_ref, x_ref, o_ref):
    y = jnp.dot(a_ref[...].astype(BF16), w_ref[...].astype(BF16), preferred_element_type=F32)
    o_ref[...] = x_ref[...] + y


def out_proj(a, w, layer, x):
    n, kf = a.shape
    d = x.shape[1]
    tm = min(PROJ_TM, n)
    return pl.pallas_call(
        _out_proj_kernel,
        grid=(n // tm, d // PROJ_TN),
        in_specs=[
            pl.BlockSpec((tm, kf), lambda i, j: (i, 0)),
            pl.BlockSpec((None, kf, PROJ_TN), lambda i, j: (layer, 0, j)),
            pl.BlockSpec((tm, PROJ_TN), lambda i, j: (i, j)),
        ],
        out_specs=pl.BlockSpec((tm, PROJ_TN), lambda i, j: (i, j)),
        out_shape=jax.ShapeDtypeStruct((n, d), F32),
        compiler_params=_params("parallel", "arbitrary"),
        name="out_proj",
    )(a, w, x)


def _ret_tables(chunk):
    lg = jnp.log1p(-jnp.exp2(-5.0 - jnp.arange(RET_HEADS, dtype=F32)))
    idx = jnp.arange(chunk, dtype=F32)
    rel = idx[:, None] - idx[None, :]
    kscale = RET_DK ** -0.5
    decay = jnp.where(rel >= 0, jnp.exp(jnp.maximum(rel, 0.0)[None] * lg[:, None, None]), 0.0) * kscale
    q_dec = jnp.exp((idx + 1.0)[None, :] * lg[:, None])[..., None]
    k_dec = jnp.exp((chunk - 1.0 - idx)[None, :] * lg[:, None])[..., None] * kscale
    s_dec = jnp.exp(chunk * lg)[:, None, None]
    return decay, q_dec, k_dec, s_dec


def _head_norm_gate(o, g):
    ms = jnp.mean(o * o, axis=-1, keepdims=True)
    return o * lax.rsqrt(ms + NORM_EPS) * _silu(g)


def _ret_prompt_kernel(q_ref, k_ref, v_ref, g_ref, dec_ref, qd_ref, kd_ref, sd_ref, og_ref, s_ref):
    @pl.when(pl.program_id(1) == 0)
    def _():
        s_ref[...] = jnp.zeros_like(s_ref)

    for h in range(RET_HEADS):
        qk_cols = slice(h * RET_DK, (h + 1) * RET_DK)
        v_cols = slice(h * RET_DV, (h + 1) * RET_DV)
        q = q_ref[:, qk_cols]
        k = k_ref[:, qk_cols]
        v = v_ref[:, v_cols]
        s = lax.dot_general(q, k, NT_DIMS, preferred_element_type=F32) * dec_ref[h]
        o = jnp.dot(s.astype(BF16), v, preferred_element_type=F32)
        state = s_ref[0, h]
        o = o + jnp.dot(q, state.astype(BF16), preferred_element_type=F32) * qd_ref[h]
        kd_t = (k.astype(F32) * kd_ref[h]).T.astype(BF16)
        s_ref[0, h] = sd_ref[h] * state + jnp.dot(kd_t, v, preferred_element_type=F32)
        og_ref[:, v_cols] = _head_norm_gate(o, g_ref[:, v_cols].astype(F32)).astype(og_ref.dtype)


def ret_prompt(proj, batch, seq):
    chunk = RET_CHUNK
    nc = seq // chunk
    decay, q_dec, k_dec, s_dec = _ret_tables(chunk)
    row = lambda b, c: b * nc + c
    const3 = lambda b, c: (0, 0, 0)
    return pl.pallas_call(
        _ret_prompt_kernel,
        grid=(batch, nc),
        in_specs=[
            pl.BlockSpec((chunk, RET_QK), lambda b, c: (row(b, c), 0)),
            pl.BlockSpec((chunk, RET_QK), lambda b, c: (row(b, c), 1)),
            pl.BlockSpec((chunk, RET_V), lambda b, c: (row(b, c), 1)),
            pl.BlockSpec((chunk, RET_V), lambda b, c: (row(b, c), 2)),
            pl.BlockSpec((RET_HEADS, chunk, chunk), const3),
            pl.BlockSpec((RET_HEADS, chunk, 1), const3),
            pl.BlockSpec((RET_HEADS, chunk, 1), const3),
            pl.BlockSpec((RET_HEADS, 1, 1), const3),
        ],
        out_specs=[
            pl.BlockSpec((chunk, RET_V), lambda b, c: (row(b, c), 0)),
            pl.BlockSpec((1, RET_HEADS, RET_DK, RET_DV), lambda b, c: (b, 0, 0, 0)),
        ],
        out_shape=[
            jax.ShapeDtypeStruct((batch * seq, RET_V), BF16),
            jax.ShapeDtypeStruct((batch, RET_HEADS, RET_DK, RET_DV), F32),
        ],
        compiler_params=_params("parallel", "arbitrary"),
        name="ret_prompt",
    )(proj, proj, proj, proj, decay, q_dec, k_dec, s_dec)


def _ret_sample_kernel(qc_ref, kc_ref, v_ref, g_ref, s0_ref, qd_ref, sd_ref, og_ref, s_ref):
    for h in range(RET_HEADS):
        qc = qc_ref[0, h]
        kc = kc_ref[0, h]
        v = v_ref[0, h]
        s0 = s0_ref[0, h]
        qk = jnp.sum(qc * kc, axis=0, keepdims=True)
        o = qk * v + jnp.sum(qc * s0, axis=0, keepdims=True) * qd_ref[h]
        s_ref[0, h] = sd_ref[h] * s0 + kc * v
        og_ref[0, h] = _head_norm_gate(o, g_ref[0, h])


def ret_sample(proj, state, layer):
    n = proj.shape[0]
    _, q_dec, k_dec, s_dec = _ret_tables(1)
    q, k, v, g = jnp.split(proj, [RET_QK, 2 * RET_QK, 2 * RET_QK + RET_V], axis=-1)
    qc = q.reshape(n, RET_HEADS, RET_DK, 1)
    kc = (k * k_dec[0, 0, 0]).reshape(n, RET_HEADS, RET_DK, 1)
    v = v.reshape(n, RET_HEADS, 1, RET_DV)
    g = g.reshape(n, RET_HEADS, 1, RET_DV)
    col_spec = pl.BlockSpec((1, RET_HEADS, RET_DK, 1), lambda b: (b, 0, 0, 0))
    row_spec = pl.BlockSpec((1, RET_HEADS, 1, RET_DV), lambda b: (b, 0, 0, 0))
    dec_spec = pl.BlockSpec((RET_HEADS, 1, 1), lambda b: (0, 0, 0))
    og, s_new = pl.pallas_call(
        _ret_sample_kernel,
        grid=(n,),
        in_specs=[
            col_spec,
            col_spec,
            row_spec,
            row_spec,
            pl.BlockSpec((None, 1, RET_HEADS, RET_DK, RET_DV), lambda b: (layer, b, 0, 0, 0)),
            dec_spec,
            dec_spec,
        ],
        out_specs=[row_spec, pl.BlockSpec((1, RET_HEADS, RET_DK, RET_DV), lambda b: (b, 0, 0, 0))],
        out_shape=[
            jax.ShapeDtypeStruct((n, RET_HEADS, 1, RET_DV), F32),
            jax.ShapeDtypeStruct((n, RET_HEADS, RET_DK, RET_DV), F32),
        ],
        compiler_params=_params("parallel"),
        name="ret_sample",
    )(qc, kc, v, g, state, q_dec, s_dec)
    return og.reshape(n, RET_V), s_new


def _alibi_slopes():
    return jnp.exp2(-8.0 * (jnp.arange(DIFF_HEADS, dtype=F32) + 1.0) / DIFF_HEADS)


def _group_mean_matrix(width):
    gid = jnp.arange(width) // DIFF_DK
    return (jnp.where(gid[:, None] == gid[None, :], 1.0 / DIFF_DK, 0.0)).astype(BF16)


def _diff_proj_kernel(x_ref, gain_ref, w_ref, gm_ref, qg_ref, kg_ref, qk_ref, kf_ref, vf_ref, vt_ref, g_ref, h_ref,
                      *, emit_vt):
    j = pl.program_id(1)

    @pl.when(j == 0)
    def _():
        _rms_to_bf16(x_ref, gain_ref, h_ref)

    y = jnp.dot(h_ref[...], w_ref[...].astype(BF16), preferred_element_type=F32)
    nq = DIFF_QK // PROJ_TN
    nv = DIFF_V // PROJ_TN

    def qk_norm(gain):
        ms = jnp.dot((y * y).astype(BF16), gm_ref[...], preferred_element_type=F32)
        return y * lax.rsqrt(ms + NORM_EPS) * gain

    @pl.when(j < nq)
    def _():
        qk_ref[...] = (qk_norm(qg_ref[...]) * (DIFF_DK ** -0.5 * LOG2E)).astype(BF16)

    @pl.when((j >= nq) & (j < 2 * nq))
    def _():
        kn = qk_norm(kg_ref[...])
        kf_ref[...] = kn
        qk_ref[...] = kn.astype(BF16)

    @pl.when((j >= 2 * nq) & (j < 2 * nq + nv))
    def _():
        vf_ref[...] = y
        if emit_vt:
            vt_ref[...] = y.T.astype(BF16)
        else:
            vt_ref[...] = y.astype(BF16)

    @pl.when(j >= 2 * nq + nv)
    def _():
        g_ref[...] = y.astype(g_ref.dtype)


def diff_proj(x, gain, w, layer, q_gain, k_gain, *, emit_vt, g_dtype):
    n, d = x.shape
    tm = min(PROJ_TM, n)
    tn = PROJ_TN
    nq = DIFF_QK // tn
    nv = DIFF_V // tn
    f = 2 * DIFF_QK + 2 * DIFF_V
    clamp = lambda j, lo, cnt: jnp.clip(j - lo, 0, cnt - 1)
    tile_gain = lambda a: jnp.tile(a, tn // DIFF_DK).reshape(1, tn)
    if emit_vt:
        vt_spec = pl.BlockSpec((tn, tm), lambda i, j: (clamp(j, 2 * nq, nv), i))
        vt_shape = jax.ShapeDtypeStruct((DIFF_V, n), BF16)
    else:
        vt_spec = pl.BlockSpec((tm, tn), lambda i, j: (i, clamp(j, 2 * nq, nv)))
        vt_shape = jax.ShapeDtypeStruct((n, DIFF_V), BF16)
    return pl.pallas_call(
        functools.partial(_diff_proj_kernel, emit_vt=emit_vt),
        grid=(n // tm, f // tn),
        in_specs=[
            pl.BlockSpec((tm, d), lambda i, j: (i, 0)),
            pl.BlockSpec((1, d), lambda i, j: (0, 0)),
            pl.BlockSpec((None, d, tn), lambda i, j: (layer, 0, j)),
            pl.BlockSpec((tn, tn), lambda i, j: (0, 0)),
            pl.BlockSpec((1, tn), lambda i, j: (0, 0)),
            pl.BlockSpec((1, tn), lambda i, j: (0, 0)),
        ],
        out_specs=[
            pl.BlockSpec((tm, tn), lambda i, j: (i, clamp(j, 0, 2 * nq))),
            pl.BlockSpec((tm, tn), lambda i, j: (i, clamp(j, nq, nq))),
            pl.BlockSpec((tm, tn), lambda i, j: (i, clamp(j, 2 * nq, nv))),
            vt_spec,
            pl.BlockSpec((tm, tn), lambda i, j: (i, clamp(j, 2 * nq + nv, nv))),
        ],
        out_shape=[
            jax.ShapeDtypeStruct((n, 2 * DIFF_QK), BF16),
            jax.ShapeDtypeStruct((n, DIFF_QK), F32),
            jax.ShapeDtypeStruct((n, DIFF_V), F32),
            vt_shape,
            jax.ShapeDtypeStruct((n, DIFF_V), g_dtype),
        ],
        scratch_shapes=[pltpu.VMEM((tm, d), BF16)],
        compiler_params=_params("parallel", "arbitrary"),
        name="diff_proj",
    )(x, gain.reshape(1, d), w, _group_mean_matrix(tn), tile_gain(q_gain), tile_gain(k_gain))


def _diff_lambda(lq1_ref, lk1_ref, lq2_ref, lk2_ref, lam_init):
    l1 = jnp.sum(lq1_ref[...] * lk1_ref[...], axis=-1, keepdims=True)
    l2 = jnp.sum(lq2_ref[...] * lk2_ref[...], axis=-1, keepdims=True)
    return jnp.exp(l1) - jnp.exp(l2) + lam_init


def _attn_kernel(slope_ref, q_ref, k_ref, vt_ref, g_ref, lq1_ref, lk1_ref, lq2_ref, lk2_ref, subg_ref, og_ref,
                 q2_ref, s_ref, m_ref, acc_ref, *, lam_init):
    tq, tk = ATTN_TQ, ATTN_TK
    qi = pl.program_id(2)
    slope = slope_ref[pl.program_id(1)]

    lane = lax.broadcasted_iota(jnp.int32, (tq, 2 * DIFF_DK), 1)
    q = q_ref[...]
    zero = jnp.zeros_like(q)
    ones_feat = jnp.where(lane < 3, 1.0, 0.0).astype(BF16)
    q2_ref[0:tq, 0:128] = jnp.where(lane < DIFF_DK, q, zero)
    q2_ref[tq:2 * tq, 0:128] = jnp.where(lane >= DIFF_DK, q, zero)
    q2_ref[0:tq, 128:256] = ones_feat
    q2_ref[tq:2 * tq, 128:256] = ones_feat

    key_lane = lax.broadcasted_iota(jnp.int32, (tk, 2 * DIFF_DK), 1)
    bias = lax.broadcasted_iota(jnp.int32, (tk, 2 * DIFF_DK), 0).astype(F32) * slope
    bias_hi = bias.astype(BF16).astype(F32)
    bias_mid = (bias - bias_hi).astype(BF16).astype(F32)
    bias_lo = bias - bias_hi - bias_mid
    key_feat = jnp.where(key_lane == 0, bias_hi, jnp.where(key_lane == 1, bias_mid,
                                                           jnp.where(key_lane == 2, bias_lo, 0.0))).astype(BF16)
    ones_rows = jnp.ones((16, tk), BF16)

    m_ref[...] = jnp.full_like(m_ref, -1e30)
    acc_ref[...] = jnp.zeros_like(acc_ref)

    def scores(ki):
        start = pl.multiple_of(ki * tk, tk)
        k_aug = jnp.concatenate([k_ref[pl.ds(start, tk), :], key_feat], axis=1)
        return lax.dot_general(k_aug, q2_ref[...], NT_DIMS, preferred_element_type=F32)

    def consume(ki, s, masked):
        start = pl.multiple_of(ki * tk, tk)
        if masked:
            key_pos = lax.broadcasted_iota(jnp.int32, (tk, 2 * tq), 0)
            col = lax.broadcasted_iota(jnp.int32, (tk, 2 * tq), 1)
            q_pos = jnp.where(col >= tq, col - tq, col)
            s = jnp.where(key_pos <= q_pos, s, -1e30)
        base = slope * (ki * tk).astype(F32)
        m_old = m_ref[...]
        m_new = jnp.maximum(m_old, jnp.max(s, axis=0, keepdims=True) + base)
        p = jnp.exp2(s + (base - m_new)).astype(BF16)
        corr = jnp.exp2(m_old - m_new)
        vt_aug = jnp.concatenate([vt_ref[:, pl.ds(start, tk)], ones_rows], axis=0)
        acc_ref[...] = acc_ref[...] * corr + jnp.dot(vt_aug, p, preferred_element_type=F32)
        m_ref[...] = m_new

    s_ref[0] = scores(0)

    def pair(j, carry):
        s_ref[1] = scores(2 * j + 1)
        consume(2 * j, s_ref[0], False)
        s_ref[0] = scores(2 * j + 2)
        consume(2 * j + 1, s_ref[1], False)
        return carry

    lax.fori_loop(0, qi // 2, pair, 0)

    @pl.when(qi % 2 == 1)
    def _():
        s_ref[1] = scores(qi)
        consume(qi - 1, s_ref[0], False)
        consume(qi, s_ref[1], True)

    @pl.when(qi % 2 == 0)
    def _():
        consume(qi, s_ref[0], True)

    lam = _diff_lambda(lq1_ref, lk1_ref, lq2_ref, lk2_ref, lam_init)
    o_t = acc_ref[0:DIFF_DV, :] / acc_ref[DIFF_DV:DIFF_DV + 1, :]
    o_t = o_t[:, 0:tq] - lam * o_t[:, tq:2 * tq]
    ms = jnp.mean(o_t * o_t, axis=0, keepdims=True)
    o_t = o_t * lax.rsqrt(ms + NORM_EPS) * subg_ref[...] * (1.0 - lam_init)
    og_ref[...] = (o_t.T * _silu(g_ref[...].astype(F32))).astype(og_ref.dtype)


def diff_attn_prompt(qk, vt, g, lam_vecs, sub_gain, batch, seq, lam_init):
    tq = ATTN_TQ
    nq = seq // tq
    row = lambda b, h, i: b * nq + i
    vec_spec = pl.BlockSpec((1, DIFF_DK), lambda b, h, i: (0, 0))
    return pl.pallas_call(
        functools.partial(_attn_kernel, lam_init=lam_init),
        grid=(batch, DIFF_HEADS, nq),
        in_specs=[
            pl.BlockSpec(memory_space=pltpu.SMEM),
            pl.BlockSpec((tq, DIFF_DV), lambda b, h, i: (row(b, h, i), h)),
            pl.BlockSpec((seq, DIFF_DV), lambda b, h, i: (b, DIFF_HEADS + h)),
            pl.BlockSpec((DIFF_DV, seq), lambda b, h, i: (h, b)),
            pl.BlockSpec((tq, DIFF_DV), lambda b, h, i: (row(b, h, i), h)),
            vec_spec,
            vec_spec,
            vec_spec,
            vec_spec,
            pl.BlockSpec((DIFF_DV, 1), lambda b, h, i: (0, 0)),
        ],
        out_specs=pl.BlockSpec((tq, DIFF_DV), lambda b, h, i: (row(b, h, i), h)),
        out_shape=jax.ShapeDtypeStruct((batch * seq, DIFF_V), BF16),
        scratch_shapes=[
            pltpu.VMEM((2 * tq, 256), BF16),
            pltpu.VMEM((2, ATTN_TK, 2 * tq), F32),
            pltpu.VMEM((1, 2 * tq), F32),
            pltpu.VMEM((DIFF_DV + 16, 2 * tq), F32),
        ],
        compiler_params=_params("parallel", "parallel", "arbitrary"),
        name="diff_attn_prompt",
    )(_alibi_slopes() * LOG2E, qk, qk, vt, g, *lam_vecs, sub_gain.reshape(DIFF_DV, 1))


def _row_to_col(row):
    return jnp.broadcast_to(row, (128, 128)).T


def _decode_kernel(pt_ref, qall_ref, kn_ref, vn_ref, g_ref, slope_ref, lq1_ref, lk1_ref, lq2_ref, lk2_ref,
                   subg_ref, *rest, past, lam_init):
    np_ = DEC_PAGES
    k_refs = rest[:np_]
    v_refs = rest[np_:2 * np_]
    og_ref, s_ref, m_ref, l_ref, acc_ref = rest[2 * np_:]
    t = pl.program_id(1)
    hd = DIFF_HEADS
    rows = PAGE_SIZE * hd

    head = lax.broadcasted_iota(jnp.int32, (hd, 128), 0)
    col = lax.broadcasted_iota(jnp.int32, (hd, 128), 1)
    valid = (col // 8 == head) & (col % 8 < 2)
    slope = slope_ref[...]

    @pl.when(t == 0)
    def _():
        s_self = jnp.dot(kn_ref[0].astype(BF16), qall_ref[0], preferred_element_type=F32)
        m_ref[...] = jnp.where(valid, s_self, 0.0)
        l_ref[...] = jnp.where(valid, 1.0, 0.0)
        sub = lax.broadcasted_iota(jnp.int32, (8, DIFF_DV), 0)
        for h in range(hd):
            acc_ref[8 * h:8 * h + 8, :] = jnp.where(sub < 2, vn_ref[0, h:h + 1, :], 0.0)

    pos = lax.broadcasted_iota(jnp.int32, (PAGE_SIZE, hd, 128), 0).astype(F32)
    pos_bias = jnp.where(valid[None], slope[None] * pos, -1e30)
    m_old = m_ref[...]
    m_new = m_old
    for p in range(np_):
        page = t * np_ + p
        s = jnp.dot(k_refs[p][0].reshape(rows, 128).astype(BF16), qall_ref[0], preferred_element_type=F32)
        page_bias = slope * ((page * PAGE_SIZE).astype(F32) - past)
        s = s.reshape(PAGE_SIZE, hd, 128) + pos_bias + page_bias[None]
        s_ref[p] = s
        m_new = jnp.maximum(m_new, jnp.max(s, axis=0))
    l = jnp.zeros((hd, 128), F32)
    pv = jnp.zeros((128, DIFF_DV), F32)
    for p in range(np_):
        e = jnp.exp2(s_ref[p] - m_new[None])
        l = l + jnp.sum(e, axis=0)
        e_t = e.reshape(rows, 128).T.astype(BF16)
        pv = pv + jnp.dot(e_t, v_refs[p][0].reshape(rows, DIFF_DV).astype(BF16), preferred_element_type=F32)
    corr = jnp.exp2(m_old - m_new)
    l_ref[...] = l_ref[...] * corr + l
    m_ref[...] = m_new
    corr_col = _row_to_col(jnp.sum(jnp.where(valid, corr, 0.0), axis=0, keepdims=True))
    acc_ref[...] = acc_ref[...] * corr_col + pv

    @pl.when(t == pl.num_programs(1) - 1)
    def _():
        l_row = jnp.sum(jnp.where(valid, l_ref[...], 0.0), axis=0, keepdims=True)
        o_all = acc_ref[...] * _row_to_col(1.0 / jnp.where(l_row > 0.0, l_row, 1.0))
        lam = _diff_lambda(lq1_ref, lk1_ref, lq2_ref, lk2_ref, lam_init)
        for h in range(hd):
            od = o_all[8 * h:8 * h + 1, :] - lam * o_all[8 * h + 1:8 * h + 2, :]
            ms = jnp.mean(od * od, axis=-1, keepdims=True)
            od = od * lax.rsqrt(ms + NORM_EPS) * subg_ref[...] * (1.0 - lam_init)
            og_ref[0, h:h + 1, :] = od * _silu(g_ref[0, h:h + 1, :])


def diff_attn_sample(qk, k_new, v_new, g, cache_k, cache_v, layer, page_table, lam_vecs, sub_gain, lam_init):
    n, n_pages = page_table.shape
    past = n_pages * PAGE_SIZE
    n_phys = cache_k.shape[1]
    hd = DIFF_HEADS
    ck = cache_k.reshape(cache_k.shape[0] * n_phys, PAGE_SIZE, hd, 2 * DIFF_DK)
    cv = cache_v.reshape(cache_v.shape[0] * n_phys, PAGE_SIZE, hd, DIFF_DV)
    pt = page_table + layer * n_phys
    col = jnp.arange(128)
    d = jnp.arange(2 * DIFF_DK)
    q_heads = qk[:, :DIFF_QK].reshape(n, hd, 2 * DIFF_DK)
    q_cols = jnp.swapaxes(q_heads[:, jnp.minimum(col // 8, hd - 1), :], 1, 2)
    keep = (col[None, :] < 8 * hd) & (col[None, :] % 8 == d[:, None] // DIFF_DK)
    q_all = jnp.where(keep[None], q_cols, jnp.zeros_like(q_cols))
    slopes = jnp.broadcast_to((_alibi_slopes() * LOG2E)[jnp.minimum(col // 8, hd - 1)], (hd, 128))
    tok_spec = pl.BlockSpec((1, hd, DIFF_DV), lambda b, t, pt: (b, 0, 0))
    vec_spec = pl.BlockSpec((1, DIFF_DK), lambda b, t, pt: (0, 0))
    page_spec = lambda p: pl.BlockSpec((1, PAGE_SIZE, hd, DIFF_DV),
                                       lambda b, t, pt: (pt[b, t * DEC_PAGES + p], 0, 0, 0))
    grid_spec = pltpu.PrefetchScalarGridSpec(
        num_scalar_prefetch=1,
        grid=(n, n_pages // DEC_PAGES),
        in_specs=[
            pl.BlockSpec((1, 128, 128), lambda b, t, pt: (b, 0, 0)),
            tok_spec,
            tok_spec,
            tok_spec,
            pl.BlockSpec((hd, 128), lambda b, t, pt: (0, 0)),
            vec_spec,
            vec_spec,
            vec_spec,
            vec_spec,
            pl.BlockSpec((1, DIFF_DV), lambda b, t, pt: (0, 0)),
        ] + [page_spec(p) for p in range(DEC_PAGES)] * 2,
        out_specs=tok_spec,
        scratch_shapes=[
            pltpu.VMEM((DEC_PAGES, PAGE_SIZE, hd, 128), F32),
            pltpu.VMEM((hd, 128), F32),
            pltpu.VMEM((hd, 128), F32),
            pltpu.VMEM((128, DIFF_DV), F32),
        ],
    )
    tok = lambda a: a.reshape(n, hd, DIFF_DV)
    og = pl.pallas_call(
        functools.partial(_decode_kernel, past=float(past), lam_init=lam_init),
        grid_spec=grid_spec,
        out_shape=jax.ShapeDtypeStruct((n, hd, DIFF_DV), F32),
        compiler_params=_params("parallel", "arbitrary"),
        name="diff_attn_sample",
    )(pt, q_all, tok(k_new), tok(v_new), tok(g), slopes, *lam_vecs, sub_gain.reshape(1, DIFF_DV),
      *([ck] * DEC_PAGES), *([cv] * DEC_PAGES))
    return og.reshape(n, DIFF_V)


def _softplus(x):
    return jnp.maximum(x, 0.0) + jnp.log1p(jnp.exp(-jnp.abs(x)))


def _lru_gates(xconv, wcat_ref, ba_ref, bx_ref, lam_ref, a_ref, b_ref):
    for n in range(LRU_BLOCKS):
        cols = slice(n * LRU_BW, (n + 1) * LRU_BW)
        xc = xconv[:, cols]
        y = jnp.dot(xc.astype(BF16), wcat_ref[n], preferred_element_type=F32)
        r = jax.nn.sigmoid(y[:, 0:LRU_BW] + ba_ref[:, cols])
        i = jax.nn.sigmoid(y[:, LRU_BW:2 * LRU_BW] + bx_ref[:, cols])
        log_a = -LRU_C * r * _softplus(-lam_ref[:, cols])
        a = jnp.exp(log_a)
        b = jnp.sqrt(-jnp.tanh(log_a) * (a * a + 1.0)) * (i * xc)
        a_ref[:, cols] = a
        b_ref[:, cols] = b


def _lru_prompt_kernel(xb_ref, g_ref, cw_ref, cb_ref, wcat_ref, ba_ref, bx_ref, lam_ref,
                       og_ref, hlast_ref, conv_ref, xpad_ref, a_ref, b_ref, h_ref):
    tt = LRU_TT
    t = pl.program_id(1)

    @pl.when(t == 0)
    def _():
        xpad_ref[0:8, :] = jnp.zeros((8, LRU_WIDTH), F32)
        h_ref[...] = jnp.zeros_like(h_ref)

    xpad_ref[8:8 + tt, :] = xb_ref[...].astype(F32)
    xconv = cb_ref[...]
    for kk in range(CONV_W):
        off = 8 - (CONV_W - 1) + kk
        xconv = xconv + xpad_ref[off:off + tt, :] * cw_ref[kk:kk + 1, :]
    _lru_gates(xconv, wcat_ref, ba_ref, bx_ref, lam_ref, a_ref, b_ref)

    def step(r, h):
        h = a_ref[pl.ds(r, 1), :] * h + b_ref[pl.ds(r, 1), :]
        b_ref[pl.ds(r, 1), :] = h
        return h

    h = lax.fori_loop(0, tt, step, h_ref[...], unroll=8)
    h_ref[...] = h
    og_ref[...] = (b_ref[...] * _silu(g_ref[...].astype(F32))).astype(og_ref.dtype)
    xpad_ref[0:8, :] = xpad_ref[tt:tt + 8, :]

    @pl.when(t == pl.num_programs(1) - 1)
    def _():
        hlast_ref[0] = h
        conv_ref[0] = xpad_ref[tt + 8 - (CONV_W - 1):tt + 8, :]


def _lru_wcat(wa, wx, layer):
    return jnp.concatenate([wa[layer], wx[layer]], axis=-1).astype(BF16)


def lru_prompt(proj, batch, seq, conv_w, conv_b, wcat, ba, bx, lam):
    tt = LRU_TT
    nt = seq // tt
    w = LRU_WIDTH
    row = lambda b, t: b * nt + t
    vec = pl.BlockSpec((1, w), lambda b, t: (0, 0))
    og, h_last, conv_new = pl.pallas_call(
        _lru_prompt_kernel,
        grid=(batch, nt),
        in_specs=[
            pl.BlockSpec((tt, w), lambda b, t: (row(b, t), 0)),
            pl.BlockSpec((tt, w), lambda b, t: (row(b, t), 1)),
            pl.BlockSpec((CONV_W, w), lambda b, t: (0, 0)),
            vec,
            pl.BlockSpec((LRU_BLOCKS, LRU_BW, 2 * LRU_BW), lambda b, t: (0, 0, 0)),
            vec,
            vec,
            vec,
        ],
        out_specs=[
            pl.BlockSpec((tt, w), lambda b, t: (row(b, t), 0)),
            pl.BlockSpec((1, 1, w), lambda b, t: (b, 0, 0)),
            pl.BlockSpec((1, CONV_W - 1, w), lambda b, t: (b, 0, 0)),
        ],
        out_shape=[
            jax.ShapeDtypeStruct((batch * seq, w), BF16),
            jax.ShapeDtypeStruct((batch, 1, w), F32),
            jax.ShapeDtypeStruct((batch, CONV_W - 1, w), F32),
        ],
        scratch_shapes=[
            pltpu.VMEM((tt + 8, w), F32),
            pltpu.VMEM((tt, w), F32),
            pltpu.VMEM((tt, w), F32),
            pltpu.VMEM((1, w), F32),
        ],
        compiler_params=_params("parallel", "arbitrary"),
        name="lru_prompt",
    )(proj, proj, conv_w, conv_b.reshape(1, w), wcat, ba.reshape(1, w), bx.reshape(1, w), lam.reshape(1, w))
    return og, h_last.reshape(batch, w), conv_new


def _lru_sample_kernel(xb_ref, g_ref, h0_ref, c0_ref, cw_ref, cb_ref, wcat_ref, ba_ref, bx_ref, lam_ref,
                       og_ref, h_ref, conv_ref, a_ref, b_ref):
    xb = xb_ref[...]
    xconv = cb_ref[...] + xb * cw_ref[CONV_W - 1:CONV_W, :]
    for kk in range(CONV_W - 1):
        xconv = xconv + c0_ref[kk] * cw_ref[kk:kk + 1, :]
    _lru_gates(xconv, wcat_ref, ba_ref, bx_ref, lam_ref, a_ref, b_ref)
    h = a_ref[...] * h0_ref[...] + b_ref[...]
    h_ref[...] = h
    og_ref[...] = h * _silu(g_ref[...])
    for kk in range(CONV_W - 2):
        conv_ref[kk] = c0_ref[kk + 1]
    conv_ref[CONV_W - 2] = xb


def lru_sample(proj, h0, conv0, conv_w, conv_b, wcat, ba, bx, lam):
    n = proj.shape[0]
    w = LRU_WIDTH
    full = lambda shape: pl.BlockSpec(shape, lambda i: (0,) * len(shape))
    og, h, conv_new = pl.pallas_call(
        _lru_sample_kernel,
        grid=(1,),
        in_specs=[
            pl.BlockSpec((n, w), lambda i: (0, 0)),
            pl.BlockSpec((n, w), lambda i: (0, 1)),
            full((n, w)),
            full((CONV_W - 1, n, w)),
            full((CONV_W, w)),
            full((1, w)),
            full((LRU_BLOCKS, LRU_BW, 2 * LRU_BW)),
            full((1, w)),
            full((1, w)),
            full((1, w)),
        ],
        out_specs=[full((n, w)), full((n, w)), full((CONV_W - 1, n, w))],
        out_shape=[
            jax.ShapeDtypeStruct((n, w), F32),
            jax.ShapeDtypeStruct((n, w), F32),
            jax.ShapeDtypeStruct((CONV_W - 1, n, w), F32),
        ],
        scratch_shapes=[pltpu.VMEM((n, w), F32), pltpu.VMEM((n, w), F32)],
        compiler_params=_params("arbitrary"),
        name="lru_sample",
    )(proj, proj, h0, jnp.swapaxes(conv0, 0, 1), conv_w, conv_b.reshape(1, w), wcat,
      ba.reshape(1, w), bx.reshape(1, w), lam.reshape(1, w))
    return og, h, jnp.swapaxes(conv_new, 0, 1)


def kernel(x_prompt, x_sample, state_ret, cache_k, cache_v, page_table, state_lru_h, state_lru_conv, norm_g,
           ret_w_in, ret_w_out, diff_w_in, diff_q_gain, diff_k_gain, diff_lam_q1, diff_lam_k1, diff_lam_q2,
           diff_lam_k2, diff_sub_gain, diff_w_out, lru_w_in, lru_conv_w, lru_conv_b, lru_wa, lru_ba, lru_wx,
           lru_bx, lru_lambda, lru_w_out):
    bp, tp, d = x_prompt.shape
    bs = x_sample.shape[0]
    assert x_sample.shape[1] == 1
    xp = x_prompt.reshape(bp * tp, d)
    xs = x_sample.reshape(bs, d)
    depth = norm_g.shape[0]
    ret_p, ret_s, kp_l, vp_l, ks_l, vs_l, hp_l, cp_l, hs_l, cs_l = ([] for _ in range(10))
    for i in range(depth):
        kind, j = i % 3, i // 3
        if kind == 0:
            proj_p = norm_proj(xp, norm_g[i], ret_w_in, j, out_dtype=BF16)
            og_p, s_p = ret_prompt(proj_p, bp, tp)
            xp = out_proj(og_p, ret_w_out, j, xp)
            proj_s = norm_proj(xs, norm_g[i], ret_w_in, j, out_dtype=F32)
            og_s, s_s = ret_sample(proj_s, state_ret, j)
            xs = out_proj(og_s, ret_w_out, j, xs)
            ret_p.append(s_p)
            ret_s.append(s_s)
        elif kind == 1:
            lam_init = 0.8 - 0.6 * math.exp(-0.3 * i)
            lam_vecs = [a[j].reshape(1, DIFF_DK) for a in (diff_lam_q1, diff_lam_k1, diff_lam_q2, diff_lam_k2)]
            qk, k_f, v_f, vt, g = diff_proj(xp, norm_g[i], diff_w_in, j, diff_q_gain[j], diff_k_gain[j],
                                            emit_vt=True, g_dtype=BF16)
            og_p = diff_attn_prompt(qk, vt, g, lam_vecs, diff_sub_gain[j], bp, tp, lam_init)
            xp = out_proj(og_p, diff_w_out, j, xp)
            kp_l.append(k_f.reshape(bp, tp, DIFF_HEADS, 2 * DIFF_DK))
            vp_l.append(v_f.reshape(bp, tp, DIFF_HEADS, DIFF_DV))
            qk_s, k_s, v_s, _, g_s = diff_proj(xs, norm_g[i], diff_w_in, j, diff_q_gain[j], diff_k_gain[j],
                                               emit_vt=False, g_dtype=F32)
            og_s = diff_attn_sample(qk_s, k_s, v_s, g_s, cache_k, cache_v, j, page_table, lam_vecs,
                                    diff_sub_gain[j], lam_init)
            xs = out_proj(og_s, diff_w_out, j, xs)
            ks_l.append(k_s.reshape(bs, 1, DIFF_HEADS, 2 * DIFF_DK))
            vs_l.append(v_s.reshape(bs, 1, DIFF_HEADS, DIFF_DV))
        else:
            wcat = _lru_wcat(lru_wa, lru_wx, j)
            lru_args = (lru_conv_w[j], lru_conv_b[j], wcat, lru_ba[j], lru_bx[j], lru_lambda[j])
            proj_p = norm_proj(xp, norm_g[i], lru_w_in, j, out_dtype=BF16)
            og_p, h_p, c_p = lru_prompt(proj_p, bp, tp, *lru_args)
            xp = out_proj(og_p, lru_w_out, j, xp)
            proj_s = norm_proj(xs, norm_g[i], lru_w_in, j, out_dtype=F32)
            og_s, h_s, c_s = lru_sample(proj_s, state_lru_h[j], state_lru_conv[j], *lru_args)
            xs = out_proj(og_s, lru_w_out, j, xs)
            hp_l.append(h_p)
            cp_l.append(c_p)
            hs_l.append(h_s)
            cs_l.append(c_s)
    return (xp.reshape(bp, tp, d), xs.reshape(bs, 1, d), jnp.stack(ret_p), jnp.stack(ret_s), jnp.stack(kp_l),
            jnp.stack(vp_l), jnp.stack(ks_l), jnp.stack(vs_l), jnp.stack(hp_l), jnp.stack(cp_l), jnp.stack(hs_l),
            jnp.stack(cs_l))
```
